```python
import jax, jax.numpy as jnp
from jax import lax
import numpy as np

D_MODEL = 2048
BATCH = 8
SEQ = 2048
DEPTH = 4

N_MIXERS = 2
CHUNK = 128
GMLP_EXPAND = 4
D_GMLP = GMLP_EXPAND * D_MODEL // 2
GMLP_GROUPS = 8
CONV_WIDTH = 31
D_FF = ((8 * D_MODEL // 3 + 255) // 256) * 256
N_EXPERTS = 8
TOP_K = 2
ALPHA = (2.0 * DEPTH) ** 0.25
BETA = (8.0 * DEPTH) ** -0.25
LN_EPS = 1e-5
N_A = (DEPTH + 1) // 2
N_B = DEPTH // 2

kernel_name = "hybrid_gmlp_conformer_moe_deepnorm_adaln"


def layer_norm(x, g, b):
    x32 = x.astype(jnp.float32)
    mu = jnp.mean(x32, axis=-1, keepdims=True)
    var = jnp.mean(jnp.square(x32 - mu), axis=-1, keepdims=True)
    return ((x32 - mu) * lax.rsqrt(var + LN_EPS) * g + b).astype(x.dtype)


def ada_modulation(cond, w, b):
    m = cond @ w + b
    shift, scale, gate = jnp.split(m, 3, axis=-1)
    return shift[:, None, :], scale[:, None, :], gate[:, None, :]


def gmlp_spatial_mixer(h, w_in, ln_g, ln_b, w_s, b_s, w_out):
    bsz, seq, _ = h.shape
    z = jax.nn.gelu(h @ w_in, approximate=False)
    u, v = jnp.split(z, 2, axis=-1)
    v = layer_norm(v, ln_g, ln_b)
    v = v.reshape(bsz, seq // CHUNK, CHUNK, GMLP_GROUPS, D_GMLP // GMLP_GROUPS)
    causal = jnp.tril(jnp.ones((CHUNK, CHUNK), dtype=bool))
    w_c = jnp.where(causal, w_s, 0)
    mixed = jnp.einsum('gts,bnsgd->bntgd', w_c, v) + b_s.T[:, :, None]
    mixed = mixed.reshape(bsz, seq, D_GMLP)
    return (u * mixed) @ w_out


def conformer_conv_mixer(h, w_pw1, dw, dw_b, ln_g, ln_b, w_pw2):
    a, g = jnp.split(h @ w_pw1, 2, axis=-1)
    y = a * jax.nn.sigmoid(g)
    y = lax.conv_general_dilated(
        y, dw[:, None, :], window_strides=(1,), padding=[(CONV_WIDTH - 1, 0)],
        dimension_numbers=('NWC', 'WIO', 'NWC'), feature_group_count=D_MODEL) + dw_b
    y = jax.nn.silu(layer_norm(y, ln_g, ln_b))
    return y @ w_pw2


def swiglu(h, w_gate, w_up, w_down):
    return (jax.nn.silu(h @ w_gate) * (h @ w_up)) @ w_down


def moe_swiglu(h, w_router, w_gate, w_up, w_down):
    bsz, seq, d = h.shape
    t = h.reshape(bsz * seq, d)
    logits = (t @ w_router).astype(jnp.float32)
    top_val, top_idx = lax.top_k(logits, TOP_K)
    top_w = jax.nn.softmax(top_val, axis=-1)
    gates = jnp.sum(jax.nn.one_hot(top_idx, N_EXPERTS, dtype=jnp.float32) * top_w[..., None], axis=1)
    gates = gates.astype(h.dtype)
    out = jnp.zeros_like(t)
    for e in range(N_EXPERTS):
        out = out + gates[:, e:e + 1] * swiglu(t, w_gate[e], w_up[e], w_down[e])
    return out.reshape(bsz, seq, d)


def setup_inputs(seed: int = 0) -> dict:
    key = jax.random.key(seed)
    ks = iter(jax.random.split(key, 32))

    def nrm(shape, std):
        return jax.random.normal(next(ks), shape, dtype=jnp.float32) * std

    d = D_MODEL
    inv = d ** -0.5
    return {
        "x": nrm((BATCH, SEQ, d), 1.0),
        "c": nrm((BATCH, d), 1.0),
        "ada_w": nrm((DEPTH, 2, d, 3 * d), 0.1 * inv),
        "ada_b": nrm((DEPTH, 2, 3 * d), 0.01),
        "post_ln_g": 1.0 + nrm((DEPTH, 2, d), 0.02),
        "post_ln_b": nrm((DEPTH, 2, d), 0.02),
        "gmlp_w_in": nrm((N_A, d, 2 * D_GMLP), inv),
        "gmlp_ln_g": 1.0 + nrm((N_A, D_GMLP), 0.02),
        "gmlp_ln_b": nrm((N_A, D_GMLP), 0.02),
        "gmlp_w_s": nrm((N_A, GMLP_GROUPS, CHUNK, CHUNK), 0.5 * CHUNK ** -0.5),
        "gmlp_b_s": 1.0 + nrm((N_A, GMLP_GROUPS, CHUNK), 0.02),
        "gmlp_w_out": nrm((N_A, D_GMLP, d), BETA * D_GMLP ** -0.5),
        "conv_w_pw1": nrm((N_B, d, 2 * d), inv),
        "conv_dw": nrm((N_B, CONV_WIDTH, d), CONV_WIDTH ** -0.5),
        "conv_dw_b": nrm((N_B, d), 0.02),
        "conv_ln_g": 1.0 + nrm((N_B, d), 0.02),
        "conv_ln_b": nrm((N_B, d), 0.02),
        "conv_w_pw2": nrm((N_B, d, d), BETA * inv),
        "ffn_w_gate": nrm((N_A, d, D_FF), inv),
        "ffn_w_up": nrm((N_A, d, D_FF), inv),
        "ffn_w_down": nrm((N_A, D_FF, d), BETA * D_FF ** -0.5),
        "moe_w_router": nrm((N_B, d, N_EXPERTS), inv),
        "moe_w_gate": nrm((N_B, N_EXPERTS, d, D_FF), inv),
        "moe_w_up": nrm((N_B, N_EXPERTS, d, D_FF), inv),
        "moe_w_down": nrm((N_B, N_EXPERTS, D_FF, d), BETA * D_FF ** -0.5),
    }


def reference(x, c, ada_w, ada_b, post_ln_g, post_ln_b,
              gmlp_w_in, gmlp_ln_g, gmlp_ln_b, gmlp_w_s, gmlp_b_s, gmlp_w_out,
              conv_w_pw1, conv_dw, conv_dw_b, conv_ln_g, conv_ln_b, conv_w_pw2,
              ffn_w_gate, ffn_w_up, ffn_w_down,
              moe_w_router, moe_w_gate, moe_w_up, moe_w_down):
    cond = jax.nn.silu(c)
    for i in range(DEPTH):
        j = i // N_MIXERS
        shift, scale, gate = ada_modulation(cond, ada_w[i, 0], ada_b[i, 0])
        h = x * (1.0 + scale) + shift
        if i % N_MIXERS == 0:
            f = gmlp_spatial_mixer(h, gmlp_w_in[j], gmlp_ln_g[j], gmlp_ln_b[j],
                                   gmlp_w_s[j], gmlp_b_s[j], gmlp_w_out[j])
        else:
            f = conformer_conv_mixer(h, conv_w_pw1[j], conv_dw[j], conv_dw_b[j],
                                     conv_ln_g[j], conv_ln_b[j], conv_w_pw2[j])
        x = layer_norm(ALPHA * x + (1.0 + gate) * f, post_ln_g[i, 0], post_ln_b[i, 0])
        shift, scale, gate = ada_modulation(cond, ada_w[i, 1], ada_b[i, 1])
        h = x * (1.0 + scale) + shift
        k = i // 2
        if i % 2 == 0:
            f = swiglu(h, ffn_w_gate[k], ffn_w_up[k], ffn_w_down[k])
        else:
            f = moe_swiglu(h, moe_w_router[k], moe_w_gate[k], moe_w_up[k], moe_w_down[k])
        x = layer_norm(ALPHA * x + (1.0 + gate) * f, post_ln_g[i, 1], post_ln_b[i, 1])
    return x
```

```python
import functools

import jax
import jax.numpy as jnp
from jax import lax
from jax.experimental import pallas as pl
from jax.experimental.pallas import tpu as pltpu

F32 = jnp.float32
BF16 = jnp.bfloat16

D_MODEL = 2048
BATCH = 8
SEQ = 2048
TOKENS = BATCH * SEQ
DEPTH = 4
CHUNK = 128
D_GMLP = 4096
GROUPS = 8
D_GROUP = D_GMLP // GROUPS
CONV_WIDTH = 31
D_FF = 5632
N_EXPERTS = 8
TOP_K = 2
ALPHA = (2.0 * DEPTH) ** 0.25
LN_EPS = 1e-5
SQRT_HALF = 0.7071067811865476

LANES = 128
HALO = 32
VMEM_LIMIT = 56 * 1024 * 1024

BM_MM = 1024
BN_GMLP = 1024
BN_FF = 512
BM_ROW = 256
BM_EXP = 512
BF_EXP = 512
N_EXP_TILES = TOP_K * TOKENS // BM_EXP + N_EXPERTS
P_ROWS = N_EXP_TILES * BM_EXP
RANK_BLK = 512
DISPATCH_CHUNK = 128


def _cparams(n_axes):
    return pltpu.CompilerParams(
        dimension_semantics=("arbitrary",) * n_axes, vmem_limit_bytes=VMEM_LIMIT)


def _resident(block_shape, index_map):
    return pl.BlockSpec(block_shape, index_map, pipeline_mode=pl.Buffered(1))


def _layer_norm(y, g, b):
    mu = jnp.mean(y, axis=-1, keepdims=True)
    yc = y - mu
    var = jnp.mean(yc * yc, axis=-1, keepdims=True)
    return yc * lax.rsqrt(var + LN_EPS) * g + b


def _post_norm(x, f, mod_ref, g_ref, b_ref):
    y = ALPHA * x + (1.0 + mod_ref[2:3, :]) * f
    return _layer_norm(y, g_ref[...], b_ref[...])


def _modulate(x, mod_ref):
    return x * (1.0 + mod_ref[1:2, :]) + mod_ref[0:1, :]


def _silu(a):
    return a * jax.nn.sigmoid(a)


def _mod_kernel(c_ref, w_ref, b_ref, o_ref):
    cond = _silu(c_ref[...])
    o_ref[...] = jnp.dot(cond, w_ref[...], precision=lax.Precision.HIGHEST,
                         preferred_element_type=F32) + b_ref[...]


def _modulation(c, ada_w, ada_b):
    n_sub = DEPTH * 2
    bn = 1536
    w = ada_w.reshape(n_sub, D_MODEL, 3 * D_MODEL)
    b = ada_b.reshape(n_sub, 1, 3 * D_MODEL)
    out = pl.pallas_call(
        _mod_kernel,
        out_shape=jax.ShapeDtypeStruct((n_sub, BATCH, 3 * D_MODEL), F32),
        grid=(n_sub, 3 * D_MODEL // bn),
        in_specs=[
            pl.BlockSpec((BATCH, D_MODEL), lambda s, n: (0, 0)),
            pl.BlockSpec((None, D_MODEL, bn), lambda s, n: (s, 0, n)),
            pl.BlockSpec((None, 1, bn), lambda s, n: (s, 0, n)),
        ],
        out_specs=pl.BlockSpec((None, BATCH, bn), lambda s, n: (s, 0, n)),
        compiler_params=_cparams(2),
        name="adaln_modulation",
    )(c, w, b)
    return out.reshape(n_sub * BATCH, 3, D_MODEL)


def _mod_spec(sub, bm):
    tiles_per_seq = SEQ // bm
    return pl.BlockSpec((None, 3, D_MODEL),
                        lambda m, *_: (sub * BATCH + m // tiles_per_seq, 0, 0))


def _first_h_kernel(x_ref, mod_ref, h_ref):
    h_ref[...] = _modulate(x_ref[...], mod_ref).astype(BF16)


def _first_h(x2, mod):
    bm = 512
    return pl.pallas_call(
        _first_h_kernel,
        out_shape=jax.ShapeDtypeStruct((TOKENS, D_MODEL), BF16),
        grid=(TOKENS // bm,),
        in_specs=[pl.BlockSpec((bm, D_MODEL), lambda m: (m, 0)), _mod_spec(0, bm)],
        out_specs=pl.BlockSpec((bm, D_MODEL), lambda m: (m, 0)),
        compiler_params=_cparams(1),
        name="first_modulate",
    )(x2, mod)


def _gmlp_in_kernel(h_ref, w_ref, z_ref):
    a = jnp.dot(h_ref[...], w_ref[...], preferred_element_type=F32)
    z_ref[...] = (0.5 * a * (1.0 + lax.erf(a * SQRT_HALF))).astype(BF16)


def _gmlp_in(h, w_in):
    bm, bn = BM_MM, BN_GMLP
    return pl.pallas_call(
        _gmlp_in_kernel,
        out_shape=jax.ShapeDtypeStruct((TOKENS, 2 * D_GMLP), BF16),
        grid=(TOKENS // bm, 2 * D_GMLP // bn),
        in_specs=[pl.BlockSpec((bm, D_MODEL), lambda m, n: (m, 0)),
                  pl.BlockSpec((D_MODEL, bn), lambda m, n: (0, n))],
        out_specs=pl.BlockSpec((bm, bn), lambda m, n: (m, n)),
        compiler_params=_cparams(2),
        name="gmlp_in_gelu",
    )(h, w_in)


def _gmlp_out_kernel(u_ref, v_ref, lng_ref, lnb_ref, ws_ref, bst_ref, wout_ref,
                     x_ref, mod_ref, pg_ref, pb_ref, modn_ref,
                     xo_ref, ho_ref, act_ref):
    bm = u_ref.shape[0]
    act_ref[...] = _layer_norm(v_ref[...].astype(F32), lng_ref[...], lnb_ref[...]).astype(BF16)
    row = lax.broadcasted_iota(jnp.int32, (CHUNK, CHUNK), 0)
    col = lax.broadcasted_iota(jnp.int32, (CHUNK, CHUNK), 1)
    causal = col <= row
    for g in range(GROUPS):
        w_c = jnp.where(causal, ws_ref[g], 0.0).astype(BF16)
        bias = bst_ref[:, g:g + 1]
        cols = slice(g * D_GROUP, (g + 1) * D_GROUP)
        for c in range(bm // CHUNK):
            rows = slice(c * CHUNK, (c + 1) * CHUNK)
            mixed = jnp.dot(w_c, act_ref[rows, cols], preferred_element_type=F32) + bias
            act_ref[rows, cols] = (u_ref[rows, cols].astype(F32) * mixed).astype(BF16)
    f = jnp.dot(act_ref[...], wout_ref[...], preferred_element_type=F32)
    xn = _post_norm(x_ref[...], f, mod_ref, pg_ref, pb_ref)
    xo_ref[...] = xn
    ho_ref[...] = _modulate(xn, modn_ref).astype(BF16)


def _gmlp_out(z, ln_g, ln_b, w_s, b_s, w_out, x2, mod, sub, post_g, post_b):
    bm = BM_ROW
    row = lambda m: (m, 0)
    fixed2 = lambda m: (0, 0)
    return pl.pallas_call(
        _gmlp_out_kernel,
        out_shape=(jax.ShapeDtypeStruct((TOKENS, D_MODEL), F32),
                   jax.ShapeDtypeStruct((TOKENS, D_MODEL), BF16)),
        grid=(TOKENS // bm,),
        in_specs=[
            pl.BlockSpec((bm, D_GMLP), lambda m: (m, 0)),
            pl.BlockSpec((bm, D_GMLP), lambda m: (m, 1)),
            _resident((1, D_GMLP), fixed2),
            _resident((1, D_GMLP), fixed2),
            _resident((GROUPS, CHUNK, CHUNK), lambda m: (0, 0, 0)),
            _resident((CHUNK, GROUPS), fixed2),
            _resident((D_GMLP, D_MODEL), fixed2),
            pl.BlockSpec((bm, D_MODEL), row),
            _mod_spec(sub, bm),
            _resident((1, D_MODEL), fixed2),
            _resident((1, D_MODEL), fixed2),
            _mod_spec(sub + 1, bm),
        ],
        out_specs=(pl.BlockSpec((bm, D_MODEL), row), pl.BlockSpec((bm, D_MODEL), row)),
        scratch_shapes=[pltpu.VMEM((bm, D_GMLP), BF16)],
        compiler_params=_cparams(1),
        name="gmlp_gate_out_norm",
    )(z, z, ln_g.reshape(1, D_GMLP), ln_b.reshape(1, D_GMLP), w_s, b_s.T, w_out,
      x2, mod, post_g.reshape(1, D_MODEL), post_b.reshape(1, D_MODEL), mod)


def _swiglu_in_kernel(h_ref, wg_ref, wu_ref, o_ref):
    h = h_ref[...]
    a = jnp.dot(h, wg_ref[...], preferred_element_type=F32)
    b = jnp.dot(h, wu_ref[...], preferred_element_type=F32)
    o_ref[...] = (_silu(a) * b).astype(BF16)


def _swiglu_in(h, w_gate, w_up):
    bm, bn = BM_MM, BN_FF
    return pl.pallas_call(
        _swiglu_in_kernel,
        out_shape=jax.ShapeDtypeStruct((TOKENS, D_FF), BF16),
        grid=(TOKENS // bm, D_FF // bn),
        in_specs=[pl.BlockSpec((bm, D_MODEL), lambda m, n: (m, 0)),
                  pl.BlockSpec((D_MODEL, bn), lambda m, n: (0, n)),
                  pl.BlockSpec((D_MODEL, bn), lambda m, n: (0, n))],
        out_specs=pl.BlockSpec((bm, bn), lambda m, n: (m, n)),
        compiler_params=_cparams(2),
        name="swiglu_in",
    )(h, w_gate, w_up)


def _down_norm_kernel(a_ref, w_ref, x_ref, mod_ref, pg_ref, pb_ref, modn_ref, xo_ref, ho_ref):
    f = jnp.dot(a_ref[...], w_ref[...], preferred_element_type=F32)
    xn = _post_norm(x_ref[...], f, mod_ref, pg_ref, pb_ref)
    xo_ref[...] = xn
    ho_ref[...] = _modulate(xn, modn_ref).astype(BF16)


def _down_norm(a, w_down, x2, mod, sub, post_g, post_b):
    bm = BM_ROW
    k = a.shape[1]
    row = lambda m: (m, 0)
    fixed2 = lambda m: (0, 0)
    return pl.pallas_call(
        _down_norm_kernel,
        out_shape=(jax.ShapeDtypeStruct((TOKENS, D_MODEL), F32),
                   jax.ShapeDtypeStruct((TOKENS, D_MODEL), BF16)),
        grid=(TOKENS // bm,),
        in_specs=[
            pl.BlockSpec((bm, k), row),
            _resident((k, D_MODEL), fixed2),
            pl.BlockSpec((bm, D_MODEL), row),
            _mod_spec(sub, bm),
            _resident((1, D_MODEL), fixed2),
            _resident((1, D_MODEL), fixed2),
            _mod_spec(sub + 1, bm),
        ],
        out_specs=(pl.BlockSpec((bm, D_MODEL), row), pl.BlockSpec((bm, D_MODEL), row)),
        compiler_params=_cparams(1),
        name="down_proj_norm",
    )(a, w_down, x2, mod, post_g.reshape(1, D_MODEL), post_b.reshape(1, D_MODEL), mod)


def _glu_in_kernel(h_ref, wa_ref, wg_ref, o_ref):
    h = h_ref[...]
    a = jnp.dot(h, wa_ref[...], preferred_element_type=F32)
    g = jnp.dot(h, wg_ref[...], preferred_element_type=F32)
    o_ref[...] = (a * jax.nn.sigmoid(g)).astype(BF16)


def _glu_in(h, w_pw1):
    bm, bn = BM_MM, BN_FF
    n_half = D_MODEL // bn
    return pl.pallas_call(
        _glu_in_kernel,
        out_shape=jax.ShapeDtypeStruct((TOKENS, D_MODEL), BF16),
        grid=(TOKENS // bm, n_half),
        in_specs=[pl.BlockSpec((bm, D_MODEL), lambda m, n: (m, 0)),
                  pl.BlockSpec((D_MODEL, bn), lambda m, n: (0, n)),
                  pl.BlockSpec((D_MODEL, bn), lambda m, n: (0, n + n_half))],
        out_specs=pl.BlockSpec((bm, bn), lambda m, n: (m, n)),
        compiler_params=_cparams(2),
        name="conv_glu_in",
    )(h, w_pw1, w_pw1)


def _conv_out_kernel(y_ref, halo_ref, dw_ref, dwb_ref, lng_ref, lnb_ref, w2_ref,
                     x_ref, mod_ref, pg_ref, pb_ref, modn_ref, wrt_ref,
                     xo_ref, ho_ref, lg_ref, ext_ref, conv_ref, act_ref):
    bm = y_ref.shape[0]
    n_lane_chunks = D_MODEL // LANES
    row_blk = 64
    first_in_seq = (pl.program_id(0) % (SEQ // bm)) == 0
    halo = jnp.where(first_in_seq, 0.0, halo_ref[...].astype(F32))
    for c in range(n_lane_chunks):
        cols = slice(c * LANES, (c + 1) * LANES)
        ext_ref[c, 0:HALO, :] = halo[:, cols]
        ext_ref[c, HALO:HALO + bm, :] = y_ref[:, cols].astype(F32)

    def conv_chunk(c, carry):
        for rb in range(bm // row_blk):
            acc = jnp.zeros((row_blk, LANES), F32)
            for k in range(CONV_WIDTH):
                start = rb * row_blk + HALO - (CONV_WIDTH - 1) + k
                acc = acc + dw_ref[c, k:k + 1, :] * ext_ref[c, start:start + row_blk, :]
            conv_ref[c, rb * row_blk:(rb + 1) * row_blk, :] = acc
        return carry

    lax.fori_loop(0, n_lane_chunks, conv_chunk, 0)
    conv = jnp.concatenate([conv_ref[c] for c in range(n_lane_chunks)], axis=1) + dwb_ref[...]
    act_ref[...] = _silu(_layer_norm(conv, lng_ref[...], lnb_ref[...])).astype(BF16)
    f = jnp.dot(act_ref[...], w2_ref[...], preferred_element_type=F32)
    xn = _post_norm(x_ref[...], f, mod_ref, pg_ref, pb_ref)
    xo_ref[...] = xn
    h = _modulate(xn, modn_ref)
    ho_ref[...] = h
    lane = lax.broadcasted_iota(jnp.int32, (bm, N_EXPERTS), 1)
    logits = jnp.zeros((bm, N_EXPERTS), F32)
    for e in range(N_EXPERTS):
        le = jnp.sum(h * wrt_ref[e:e + 1, :], axis=-1, keepdims=True)
        logits = jnp.where(lane == e, le, logits)
    lg_ref[...] = logits


def _conv_out(y, dw, dw_b, ln_g, ln_b, w_pw2, x2, mod, sub, post_g, post_b, w_router):
    bm = BM_ROW
    row = lambda m: (m, 0)
    fixed2 = lambda m: (0, 0)
    halo_per_tile = bm // HALO
    n_lane_chunks = D_MODEL // LANES
    dw3 = dw.reshape(CONV_WIDTH, n_lane_chunks, LANES).transpose(1, 0, 2)
    return pl.pallas_call(
        _conv_out_kernel,
        out_shape=(jax.ShapeDtypeStruct((TOKENS, D_MODEL), F32),
                   jax.ShapeDtypeStruct((TOKENS, D_MODEL), F32),
                   jax.ShapeDtypeStruct((TOKENS, N_EXPERTS), F32)),
        grid=(TOKENS // bm,),
        in_specs=[
            pl.BlockSpec((bm, D_MODEL), row),
            pl.BlockSpec((HALO, D_MODEL), lambda m: (jnp.maximum(m * halo_per_tile - 1, 0), 0)),
            _resident((n_lane_chunks, CONV_WIDTH, LANES), lambda m: (0, 0, 0)),
            _resident((1, D_MODEL), fixed2),
            _resident((1, D_MODEL), fixed2),
            _resident((1, D_MODEL), fixed2),
            _resident((D_MODEL, D_MODEL), fixed2),
            pl.BlockSpec((bm, D_MODEL), row),
            _mod_spec(sub, bm),
            _resident((1, D_MODEL), fixed2),
            _resident((1, D_MODEL), fixed2),
            _mod_spec(sub + 1, bm),
            _resident((N_EXPERTS, D_MODEL), fixed2),
        ],
        out_specs=(pl.BlockSpec((bm, D_MODEL), row), pl.BlockSpec((bm, D_MODEL), row),
                   pl.BlockSpec((bm, N_EXPERTS), row)),
        scratch_shapes=[pltpu.VMEM((n_lane_chunks, HALO + bm, LANES), F32),
                        pltpu.VMEM((n_lane_chunks, bm, LANES), F32),
                        pltpu.VMEM((bm, D_MODEL), BF16)],
        compiler_params=_cparams(1),
        name="conv_norm_out_norm",
    )(y, y, dw3, dw_b.reshape(1, D_MODEL), ln_g.reshape(1, D_MODEL), ln_b.reshape(1, D_MODEL),
      w_pw2, x2, mod, post_g.reshape(1, D_MODEL), post_b.reshape(1, D_MODEL), mod, w_router.T)


def _route_kernel(lt_ref, pos_ref, wt_ref, tile_ref, cnt_ref, off_ref, rank_ref):
    l = lt_ref[...]
    e_iota = lax.broadcasted_iota(jnp.int32, (N_EXPERTS, TOKENS), 0)
    m1 = jnp.max(l, axis=0, keepdims=True)
    i1 = jnp.min(jnp.where(l == m1, e_iota, N_EXPERTS), axis=0, keepdims=True)
    sel1 = e_iota == i1
    l2 = jnp.where(sel1, -jnp.inf, l)
    m2 = jnp.max(l2, axis=0, keepdims=True)
    i2 = jnp.min(jnp.where(l2 == m2, e_iota, N_EXPERTS), axis=0, keepdims=True)
    sel2 = e_iota == i2
    ex = jnp.exp(m2 - m1)
    den = 1.0 + ex
    wt_ref[0:1, :] = 1.0 / den
    wt_ref[1:2, :] = ex / den

    r_i = lax.broadcasted_iota(jnp.int32, (RANK_BLK, RANK_BLK), 0)
    c_i = lax.broadcasted_iota(jnp.int32, (RANK_BLK, RANK_BLK), 1)
    strict_upper = (r_i < c_i).astype(BF16)
    count = jnp.zeros((N_EXPERTS, 1), F32)
    for b in range(TOKENS // RANK_BLK):
        lanes = slice(b * RANK_BLK, (b + 1) * RANK_BLK)
        mb = (sel1[:, lanes] | sel2[:, lanes]).astype(F32)
        rank_ref[:, lanes] = jnp.dot(mb.astype(BF16), strict_upper,
                                     preferred_element_type=F32) + count
        count = count + jnp.sum(mb, axis=1, keepdims=True)

    padded = jnp.floor((count + (BM_EXP - 1)) * (1.0 / BM_EXP)) * BM_EXP
    e_col = lax.broadcasted_iota(jnp.int32, (N_EXPERTS, 1), 0)
    off = jnp.zeros((N_EXPERTS, 1), F32)
    for e in range(N_EXPERTS - 1):
        off = off + jnp.where(e_col > e, padded[e:e + 1, :], 0.0)
    dest = off + rank_ref[...]
    pos_ref[0:1, :] = jnp.sum(jnp.where(sel1, dest, 0.0), axis=0, keepdims=True).astype(jnp.int32)
    pos_ref[1:2, :] = jnp.sum(jnp.where(sel2, dest, 0.0), axis=0, keepdims=True).astype(jnp.int32)

    start = (lax.broadcasted_iota(jnp.int32, (1, LANES), 1) * BM_EXP).astype(F32)
    ends = off + padded
    tile_e = jnp.sum((ends <= start).astype(F32), axis=0, keepdims=True)
    tile_ref[0:1, :] = jnp.minimum(tile_e, N_EXPERTS - 1.0).astype(jnp.int32)
    n_used = jnp.sum(padded, axis=0, keepdims=True) * (1.0 / BM_EXP)
    tile_ref[1:2, :] = jnp.broadcast_to(n_used, (1, LANES)).astype(jnp.int32)
    cnt_ref[...] = jnp.broadcast_to(count, (N_EXPERTS, LANES)).astype(jnp.int32)
    off_ref[...] = jnp.broadcast_to(off, (N_EXPERTS, LANES)).astype(jnp.int32)


def _route(logits_t):
    assert N_EXP_TILES <= LANES
    vm = pl.BlockSpec(memory_space=pltpu.VMEM)
    return pl.pallas_call(
        _route_kernel,
        out_shape=(jax.ShapeDtypeStruct((TOP_K, TOKENS), jnp.int32),
                   jax.ShapeDtypeStruct((TOP_K, TOKENS), F32),
                   jax.ShapeDtypeStruct((2, LANES), jnp.int32),
                   jax.ShapeDtypeStruct((N_EXPERTS, LANES), jnp.int32),
                   jax.ShapeDtypeStruct((N_EXPERTS, LANES), jnp.int32)),
        in_specs=[vm],
        out_specs=(vm, vm, vm, vm, vm),
        scratch_shapes=[pltpu.VMEM((N_EXPERTS, TOKENS), F32)],
        compiler_params=pltpu.CompilerParams(vmem_limit_bytes=VMEM_LIMIT),
        name="moe_route",
    )(logits_t)


def _dispatch_kernel(pos_ref, cnt_ref, off_ref, nu_ref, h_ref, xs_ref, zero_ref, sem, pad_sem):
    n_chunks = TOKENS // DISPATCH_CHUNK

    def row_copy(src_row, dst_row, s):
        return pltpu.make_async_copy(h_ref.at[pl.ds(src_row, 1)], xs_ref.at[pl.ds(dst_row, 1)], s)

    def zero_tile(tile_row):
        rows = pl.ds(pl.multiple_of(tile_row, BM_EXP), BM_EXP)
        return pltpu.make_async_copy(zero_ref, xs_ref.at[rows], pad_sem)

    zero_ref[...] = jnp.zeros_like(zero_ref)
    partial = [cnt_ref[e] % BM_EXP != 0 for e in range(N_EXPERTS)]
    last_tile = [off_ref[e] + (cnt_ref[e] // BM_EXP) * BM_EXP for e in range(N_EXPERTS)]
    unused = [nu_ref[0] + i < N_EXP_TILES for i in range(N_EXPERTS)]
    spare_tile = [jnp.minimum(nu_ref[0] + i, N_EXP_TILES - 1) * BM_EXP for i in range(N_EXPERTS)]
    for wait in (False, True):
        for cond, tile_row in zip(partial + unused, last_tile + spare_tile):
            @pl.when(cond)
            def _():
                if wait:
                    zero_tile(tile_row).wait()
                else:
                    zero_tile(tile_row).start()

    def issue(c):
        s = sem.at[c % 2]

        def body(j, carry):
            t = c * DISPATCH_CHUNK + j
            for k in range(TOP_K):
                row_copy(t, pos_ref[k * TOKENS + t], s).start()
            return carry

        lax.fori_loop(0, DISPATCH_CHUNK, body, 0)

    def drain(c):
        s = sem.at[c % 2]

        def body(j, carry):
            row_copy(0, 0, s).wait()
            return carry

        lax.fori_loop(0, TOP_K * DISPATCH_CHUNK, body, 0)

    issue(0)

    def step(c, carry):
        issue(c)
        drain(c - 1)
        return carry

    lax.fori_loop(1, n_chunks, step, 0)
    drain(n_chunks - 1)


def _dispatch(pos_flat, cnt, off, n_used, h32):
    any_spec = pl.BlockSpec(memory_space=pl.ANY)
    return pl.pallas_call(
        _dispatch_kernel,
        out_shape=jax.ShapeDtypeStruct((P_ROWS, D_MODEL), F32),
        grid_spec=pltpu.PrefetchScalarGridSpec(
            num_scalar_prefetch=4,
            grid=(1,),
            in_specs=[any_spec],
            out_specs=any_spec,
            scratch_shapes=[pltpu.VMEM((BM_EXP, D_MODEL), F32),
                            pltpu.SemaphoreType.DMA((2,)), pltpu.SemaphoreType.DMA(())],
        ),
        compiler_params=_cparams(1),
        name="moe_dispatch",
    )(pos_flat, cnt, off, n_used, h32)


def _expert_kernel(te_ref, nu_ref, x_ref, wg_ref, wu_ref, wd_ref, y_ref, xb_ref, acc_ref):
    i = pl.program_id(0)
    f = pl.program_id(1)

    @pl.when(i < nu_ref[0])
    def _():
        @pl.when(f == 0)
        def _():
            xb_ref[...] = x_ref[...].astype(BF16)
            acc_ref[...] = jnp.zeros_like(acc_ref)

        xb = xb_ref[...]
        a = jnp.dot(xb, wg_ref[...], preferred_element_type=F32)
        b = jnp.dot(xb, wu_ref[...], preferred_element_type=F32)
        p = (_silu(a) * b).astype(BF16)
        acc_ref[...] += jnp.dot(p, wd_ref[...], preferred_element_type=F32)

        @pl.when(f == pl.num_programs(1) - 1)
        def _():
            y_ref[...] = acc_ref[...]

    @pl.when(jnp.logical_and(i >= nu_ref[0], f == 0))
    def _():
        y_ref[...] = jnp.zeros_like(y_ref)


def _experts(tile_expert, n_used, xs, w_gate, w_up, w_down):
    n_f = D_FF // BF_EXP

    def row_map(i, f, te, nu):
        return (jnp.minimum(i, nu[0] - 1), 0)

    def _clamped(i, f, te, nu):
        live = i < nu[0]
        return te[jnp.minimum(i, nu[0] - 1)], jnp.where(live, f, n_f - 1)

    def up_map(i, f, te, nu):
        e, ff = _clamped(i, f, te, nu)
        return (e, 0, ff)

    def down_map(i, f, te, nu):
        e, ff = _clamped(i, f, te, nu)
        return (e, ff, 0)

    return pl.pallas_call(
        _expert_kernel,
        out_shape=jax.ShapeDtypeStruct((P_ROWS, D_MODEL), F32),
        grid_spec=pltpu.PrefetchScalarGridSpec(
            num_scalar_prefetch=2,
            grid=(N_EXP_TILES, n_f),
            in_specs=[
                pl.BlockSpec((BM_EXP, D_MODEL), row_map),
                pl.BlockSpec((None, D_MODEL, BF_EXP), up_map),
                pl.BlockSpec((None, D_MODEL, BF_EXP), up_map),
                pl.BlockSpec((None, BF_EXP, D_MODEL), down_map),
            ],
            out_specs=pl.BlockSpec((BM_EXP, D_MODEL), lambda i, f, te, nu: (i, 0)),
            scratch_shapes=[pltpu.VMEM((BM_EXP, D_MODEL), BF16),
                            pltpu.VMEM((BM_EXP, D_MODEL), F32)],
        ),
        compiler_params=_cparams(2),
        name="moe_experts",
    )(tile_expert, n_used, xs, w_gate, w_up, w_down)


def _combine_kernel(pos_ref, ys_ref, wt_ref, x_ref, mod_ref, pg_ref, pb_ref, modn_ref,
                    *rest, has_next):
    if has_next:
        xo_ref, ho_ref, buf_ref, sem = rest
    else:
        xo_ref, buf_ref, sem = rest
    bm = x_ref.shape[0]
    base = pl.program_id(0) * bm

    def issue(j, carry):
        for k in range(TOP_K):
            p = pos_ref[k * TOKENS + base + j]
            pltpu.make_async_copy(ys_ref.at[pl.ds(p, 1)], buf_ref.at[k, pl.ds(j, 1)], sem).start()
        return carry

    def drain(j, carry):
        pltpu.make_async_copy(ys_ref.at[pl.ds(0, 1)], buf_ref.at[0, pl.ds(0, 1)], sem).wait()
        return carry

    lax.fori_loop(0, bm, issue, 0)
    lax.fori_loop(0, TOP_K * bm, drain, 0)
    f = wt_ref[:, 0:1] * buf_ref[0] + wt_ref[:, 1:2] * buf_ref[1]
    xn = _post_norm(x_ref[...], f, mod_ref, pg_ref, pb_ref)
    xo_ref[...] = xn
    if has_next:
        ho_ref[...] = _modulate(xn, modn_ref).astype(BF16)


def _combine(pos_flat, ys, wts_t, x2, mod, sub, post_g, post_b, has_next):
    bm = BM_ROW
    row = lambda m, pos: (m, 0)
    fixed2 = lambda m, pos: (0, 0)
    out_shape = [jax.ShapeDtypeStruct((TOKENS, D_MODEL), F32)]
    out_specs = [pl.BlockSpec((bm, D_MODEL), row)]
    if has_next:
        out_shape.append(jax.ShapeDtypeStruct((TOKENS, D_MODEL), BF16))
        out_specs.append(pl.BlockSpec((bm, D_MODEL), row))
    next_sub = sub + 1 if has_next else sub
    return pl.pallas_call(
        functools.partial(_combine_kernel, has_next=has_next),
        out_shape=tuple(out_shape),
        grid_spec=pltpu.PrefetchScalarGridSpec(
            num_scalar_prefetch=1,
            grid=(TOKENS // bm,),
            in_specs=[
                pl.BlockSpec(memory_space=pl.ANY),
                pl.BlockSpec((bm, TOP_K), row),
                pl.BlockSpec((bm, D_MODEL), row),
                _mod_spec(sub, bm),
                _resident((1, D_MODEL), fixed2),
                _resident((1, D_MODEL), fixed2),
                _mod_spec(next_sub, bm),
            ],
            out_specs=tuple(out_specs),
            scratch_shapes=[pltpu.VMEM((TOP_K, bm, D_MODEL), F32), pltpu.SemaphoreType.DMA(())],
        ),
        compiler_params=_cparams(1),
        name="moe_combine_norm",
    )(pos_flat, ys, wts_t, x2, mod, post_g.reshape(1, D_MODEL), post_b.reshape(1, D_MODEL), mod)


def _moe(h32, logits, w_gate, w_up, w_down, x2, mod, sub, post_g, post_b, has_next):
    pos, wts, tiles, cnt, off = _route(logits.T)
    pos_flat = pos.reshape(TOP_K * TOKENS)
    xs = _dispatch(pos_flat, cnt[:, 0], off[:, 0], tiles[1, :1], h32)
    ys = _experts(tiles[0, :N_EXP_TILES], tiles[1, :1], xs, w_gate, w_up, w_down)
    return _combine(pos_flat, ys, wts.T, x2, mod, sub, post_g, post_b, has_next)


def kernel(x, c, ada_w, ada_b, post_ln_g, post_ln_b, gmlp_w_in, gmlp_ln_g, gmlp_ln_b, gmlp_w_s,
           gmlp_b_s, gmlp_w_out, conv_w_pw1, conv_dw, conv_dw_b, conv_ln_g, conv_ln_b, conv_w_pw2,
           ffn_w_gate, ffn_w_up, ffn_w_down, moe_w_router, moe_w_gate, moe_w_up, moe_w_down):
    bf = lambda w: w.astype(BF16)
    mod = _modulation(c, ada_w, ada_b)
    x2 = x.reshape(TOKENS, D_MODEL)
    h = _first_h(x2, mod)
    for i in range(DEPTH):
        j = i // 2
        sub = 2 * i
        if i % 2 == 0:
            z = _gmlp_in(h, bf(gmlp_w_in[j]))
            x2, h = _gmlp_out(z, gmlp_ln_g[j], gmlp_ln_b[j], gmlp_w_s[j], gmlp_b_s[j],
                              bf(gmlp_w_out[j]), x2, mod, sub, post_ln_g[i, 0], post_ln_b[i, 0])
            a = _swiglu_in(h, bf(ffn_w_gate[j]), bf(ffn_w_up[j]))
            x2, h = _down_norm(a, bf(ffn_w_down[j]), x2, mod, sub + 1,
                               post_ln_g[i, 1], post_ln_b[i, 1])
        else:
            y = _glu_in(h, bf(conv_w_pw1[j]))
            x2, h32, logits = _conv_out(y, conv_dw[j], conv_dw_b[j], conv_ln_g[j], conv_ln_b[j],
                                        bf(conv_w_pw2[j]), x2, mod, sub,
                                        post_ln_g[i, 0], post_ln_b[i, 0], moe_w_router[j])
            has_next = i + 1 < DEPTH
            out = _moe(h32, logits, bf(moe_w_gate[j]), bf(moe_w_up[j]), bf(moe_w_down[j]),
                       x2, mod, sub + 1, post_ln_g[i, 1], post_ln_b[i, 1], has_next)
            if has_next:
                x2, h = out
            else:
                (x2,) = out
    return x2.reshape(BATCH, SEQ, D_MODEL)
```
